```python
import math
import jax, jax.numpy as jnp
from jax import lax
import numpy as np

D_MODEL = 1024
BATCH = 1
SEQ = 16384
DEPTH = 1
DEC_BATCH = 8
DEC_SEQ = 32
PAST_LEN = 2048

CHUNK = 64
RMS_EPS = 1e-6
D_INNER = 2 * D_MODEL
D_SSD = D_INNER // 2
D_MLSTM = D_INNER - D_SSD
SSD_HEAD_DIM = 64
SSD_HEADS = D_SSD // SSD_HEAD_DIM
SSD_GROUPS = 4
SSD_HPG = SSD_HEADS // SSD_GROUPS
SSD_STATE = 128
SSD_CONV = 4
SSD_CONV_DIM = D_SSD + 2 * SSD_GROUPS * SSD_STATE
MLSTM_HEADS = 4
MLSTM_HEAD_DIM = D_MLSTM // MLSTM_HEADS
OFF_Z = 0
OFF_XBC = OFF_Z + D_SSD
OFF_DT = OFF_XBC + SSD_CONV_DIM
OFF_Q = OFF_DT + SSD_HEADS
OFF_K = OFF_Q + D_MLSTM
OFF_V = OFF_K + D_MLSTM
OFF_O = OFF_V + D_MLSTM
OFF_IF = OFF_O + D_MLSTM
N_IN = OFF_IF + 2 * MLSTM_HEADS
PEER_HEADS = 8
PEER_KEYS = 128
PEER_EXPERTS = PEER_KEYS * PEER_KEYS
PEER_TOPK = 16
PEER_KEY_DIM = 256
PEER_HALF = PEER_KEY_DIM // 2
PEER_BLOCK = 128

kernel_name = "hybrid_ssd_mlstm_peer_stream_step"


def rms_norm(x, w):
    x32 = x.astype(jnp.float32)
    y = x32 * lax.rsqrt(jnp.mean(x32 * x32, axis=-1, keepdims=True) + RMS_EPS)
    return (y * w.astype(jnp.float32)).astype(x.dtype)


def group_rms(x32, w, groups):
    shp = x32.shape
    xg = x32.reshape(shp[:-1] + (groups, shp[-1] // groups))
    xg = xg * lax.rsqrt(jnp.mean(xg * xg, axis=-1, keepdims=True) + RMS_EPS)
    return xg.reshape(shp) * w.astype(jnp.float32)


def to_chunks(a, q):
    b, l = a.shape[:2]
    return jnp.moveaxis(a.reshape((b, l // q, q) + a.shape[2:]), 1, 0)


def from_chunks(a):
    a = jnp.moveaxis(a, 0, 1)
    return a.reshape((a.shape[0], a.shape[1] * a.shape[2]) + a.shape[3:])


def causal_conv(xbc, buf, w, b):
    full = jnp.concatenate([buf.astype(xbc.dtype), xbc], axis=1)
    out = lax.conv_general_dilated(full, w[:, None, :], window_strides=(1,), padding='VALID',
                                   dimension_numbers=('NWC', 'WIO', 'NWC'),
                                   feature_group_count=SSD_CONV_DIM)
    return out + b, full[:, -(SSD_CONV - 1):]


def ssd_mixer(xbc, z, dt_raw, state_h, a_log, dt_bias, d_skip, norm_w):
    bsz, l = xbc.shape[:2]
    f32 = jnp.float32
    xbc = xbc.astype(f32)
    xs = xbc[..., :D_SSD].reshape(bsz, l, SSD_GROUPS, SSD_HPG, SSD_HEAD_DIM)
    bm = xbc[..., D_SSD:D_SSD + SSD_GROUPS * SSD_STATE].reshape(bsz, l, SSD_GROUPS, SSD_STATE)
    cm = xbc[..., D_SSD + SSD_GROUPS * SSD_STATE:].reshape(bsz, l, SSD_GROUPS, SSD_STATE)
    dt = jax.nn.softplus(dt_raw.astype(f32) + dt_bias.astype(f32)).reshape(bsz, l, SSD_GROUPS, SSD_HPG)
    a = -jnp.exp(a_log.astype(f32)).reshape(SSD_GROUPS, SSD_HPG)
    q = min(CHUNK, l)
    tril = jnp.tril(jnp.ones((q, q), dtype=bool))

    def body(s, inp):
        xc, dtc, bc, cc = inp
        cum = jnp.cumsum(dtc * a, axis=1)
        ct = jnp.moveaxis(cum, 1, -1)
        seg = jnp.exp(jnp.where(tril, ct[..., :, None] - ct[..., None, :], -jnp.inf))
        xdt = xc * dtc[..., None]
        cb = jnp.einsum('bqgn,bsgn->bgqs', cc, bc)
        y = jnp.einsum('bgqs,bghqs,bsghp->bqghp', cb, seg, xdt)
        y = y + jnp.einsum('bqgn,bghpn->bqghp', cc, s) * jnp.exp(cum)[..., None]
        to_end = jnp.exp(cum[:, -1:] - cum)
        s_new = (jnp.exp(cum[:, -1])[..., None, None] * s
                 + jnp.einsum('bsgn,bsgh,bsghp->bghpn', bc, to_end, xdt))
        return s_new, y

    s0 = state_h.astype(f32).reshape(bsz, SSD_GROUPS, SSD_HPG, SSD_HEAD_DIM, SSD_STATE)
    s_fin, ys = lax.scan(body, s0, (to_chunks(xs, q), to_chunks(dt, q),
                                    to_chunks(bm, q), to_chunks(cm, q)))
    y = from_chunks(ys) + d_skip.astype(f32).reshape(SSD_GROUPS, SSD_HPG)[..., None] * xs
    y = y.reshape(bsz, l, D_SSD) * jax.nn.silu(z.astype(f32))
    y = group_rms(y, norm_w, SSD_GROUPS)
    return y, s_fin.reshape(bsz, SSD_HEADS, SSD_HEAD_DIM, SSD_STATE)


def mlstm_mixer(q_in, k_in, v_in, o_raw, if_raw, c0, n0, m0, if_bias, norm_w):
    bsz, l = q_in.shape[:2]
    f32 = jnp.float32
    hd = (bsz, l, MLSTM_HEADS, MLSTM_HEAD_DIM)
    qh = q_in.astype(f32).reshape(hd)
    kh = k_in.astype(f32).reshape(hd) * (MLSTM_HEAD_DIM ** -0.5)
    vh = v_in.astype(f32).reshape(hd)
    pre = if_raw.astype(f32) + if_bias.astype(f32)
    ig = pre[..., :MLSTM_HEADS]
    lf = jax.nn.log_sigmoid(pre[..., MLSTM_HEADS:])
    q = min(CHUNK, l)
    tril = jnp.tril(jnp.ones((q, q), dtype=bool))

    def body(carry, inp):
        c, n, m = carry
        qc, kc, vc, ic, fc = inp
        bt = jnp.cumsum(fc, axis=1).transpose(0, 2, 1)
        it = ic.transpose(0, 2, 1)
        d = jnp.where(tril, bt[..., :, None] - bt[..., None, :] + it[..., None, :], -jnp.inf)
        inter = bt + m[..., None]
        m_t = jnp.maximum(inter, d.max(-1))
        w_intra = jnp.exp(d - m_t[..., None])
        w_inter = jnp.exp(inter - m_t)
        s = jnp.einsum('bqhd,bshd->bhqs', qc, kc) * w_intra
        num = (jnp.einsum('bhqs,bshv->bqhv', s, vc)
               + jnp.einsum('bqhk,bhkv->bqhv', qc, c) * w_inter.transpose(0, 2, 1)[..., None])
        den = s.sum(-1) + w_inter * jnp.einsum('bqhk,bhk->bhq', qc, n)
        den = jnp.maximum(jnp.abs(den), jnp.exp(-m_t))
        h = num / den.transpose(0, 2, 1)[..., None]
        b_end = bt[..., -1]
        g = b_end[..., None] - bt + it
        m_new = jnp.maximum(b_end + m, g.max(-1))
        w = jnp.exp(g - m_new[..., None])
        decay = jnp.exp(b_end + m - m_new)
        c_new = decay[..., None, None] * c + jnp.einsum('bhs,bshk,bshv->bhkv', w, kc, vc)
        n_new = decay[..., None] * n + jnp.einsum('bhs,bshk->bhk', w, kc)
        return (c_new, n_new, m_new), h

    carry0 = (c0.astype(f32), n0.astype(f32), m0.astype(f32))
    (c_f, n_f, m_f), hs = lax.scan(body, carry0, (to_chunks(qh, q), to_chunks(kh, q), to_chunks(vh, q),
                                                  to_chunks(ig, q), to_chunks(lf, q)))
    h = from_chunks(hs)
    h = h * lax.rsqrt(jnp.mean(h * h, axis=-1, keepdims=True) + RMS_EPS)
    h = h * norm_w.astype(f32).reshape(MLSTM_HEADS, MLSTM_HEAD_DIM)
    h = h.reshape(bsz, l, D_MLSTM) * jax.nn.sigmoid(o_raw.astype(f32))
    return h, c_f, n_f, m_f


def peer(h, w_q, sub_keys, u_emb, v_emb):
    bsz, l, d = h.shape
    t = bsz * l
    nb = -(-t // PEER_BLOCK)
    flat = jnp.pad(h.reshape(t, d), ((0, nb * PEER_BLOCK - t), (0, 0))).reshape(nb, PEER_BLOCK, d)

    def block(xb):
        qv = (xb @ w_q).reshape(PEER_BLOCK, PEER_HEADS, 2, PEER_HALF)
        s = jnp.einsum('thcd,hcnd->thcn', qv, sub_keys).astype(jnp.float32)
        s1, i1 = lax.top_k(s[:, :, 0], PEER_TOPK)
        s2, i2 = lax.top_k(s[:, :, 1], PEER_TOPK)
        cand_s = (s1[..., :, None] + s2[..., None, :]).reshape(PEER_BLOCK, PEER_HEADS, PEER_TOPK * PEER_TOPK)
        cand_i = (i1[..., :, None] * PEER_KEYS + i2[..., None, :]).reshape(PEER_BLOCK, PEER_HEADS, PEER_TOPK * PEER_TOPK)
        top_s, pos = lax.top_k(cand_s, PEER_TOPK)
        idx = jnp.take_along_axis(cand_i, pos, axis=-1)
        gate = jax.nn.softmax(top_s, axis=-1)
        a = jnp.einsum('td,thkd->thk', xb, u_emb[idx])
        act = (jax.nn.gelu(a.astype(jnp.float32), approximate=False) * gate).astype(xb.dtype)
        return jnp.einsum('thk,thkd->td', act, v_emb[idx])

    out = lax.map(block, flat)
    return out.reshape(nb * PEER_BLOCK, d)[:t].reshape(bsz, l, d)


def layer(x, ssd_h, ssd_conv, m_c, m_n, m_m, norm_mix_w, w_in, conv_w, conv_b, dt_bias, a_log,
          d_skip, ssd_norm_w, if_bias, mlstm_norm_w, w_out, norm_ffn_w, peer_w_q, peer_sub_keys,
          peer_u, peer_v):
    h = rms_norm(x, norm_mix_w)
    proj = h @ w_in
    z = proj[..., OFF_Z:OFF_XBC]
    xbc, new_conv = causal_conv(proj[..., OFF_XBC:OFF_DT], ssd_conv, conv_w, conv_b)
    xbc = jax.nn.silu(xbc)
    y_ssd, new_h = ssd_mixer(xbc, z, proj[..., OFF_DT:OFF_Q], ssd_h, a_log, dt_bias, d_skip, ssd_norm_w)
    y_ml, c_f, n_f, m_f = mlstm_mixer(proj[..., OFF_Q:OFF_K], proj[..., OFF_K:OFF_V], proj[..., OFF_V:OFF_O],
                                      proj[..., OFF_O:OFF_IF], proj[..., OFF_IF:], m_c, m_n, m_m,
                                      if_bias, mlstm_norm_w)
    mix = jnp.concatenate([y_ssd, y_ml], axis=-1).astype(x.dtype)
    x = x + mix @ w_out
    x = x + peer(rms_norm(x, norm_ffn_w), peer_w_q, peer_sub_keys, peer_u, peer_v)
    dt_ = x.dtype
    return x, (new_h.astype(dt_), new_conv.astype(dt_), c_f.astype(dt_), n_f.astype(dt_), m_f.astype(dt_))


def setup_inputs(seed: int = 0) -> dict:
    key = jax.random.key(seed)
    ks = jax.random.split(key, 32)
    f32 = jnp.float32

    def nrm(k, shape, scale):
        return jax.random.normal(k, shape, f32) * scale

    dt0 = jnp.exp(jax.random.uniform(ks[10], (DEPTH, SSD_HEADS), f32, math.log(1e-3), math.log(1e-1)))
    if_bias = jnp.concatenate([nrm(ks[12], (DEPTH, MLSTM_HEADS), 0.1) - 1.0,
                               nrm(ks[13], (DEPTH, MLSTM_HEADS), 0.3) + 3.0], axis=-1)
    return {
        "x_prompt": nrm(ks[0], (BATCH, SEQ, D_MODEL), 1.0),
        "x_sample": nrm(ks[1], (DEC_BATCH, DEC_SEQ, D_MODEL), 1.0),
        "state_ssd_h": nrm(ks[2], (DEPTH, DEC_BATCH, SSD_HEADS, SSD_HEAD_DIM, SSD_STATE), 0.1),
        "state_ssd_conv": nrm(ks[3], (DEPTH, DEC_BATCH, SSD_CONV - 1, SSD_CONV_DIM), 1.0),
        "state_mlstm_c": nrm(ks[4], (DEPTH, DEC_BATCH, MLSTM_HEADS, MLSTM_HEAD_DIM, MLSTM_HEAD_DIM), 0.1),
        "state_mlstm_n": nrm(ks[5], (DEPTH, DEC_BATCH, MLSTM_HEADS, MLSTM_HEAD_DIM), 0.1),
        "state_mlstm_m": nrm(ks[6], (DEPTH, DEC_BATCH, MLSTM_HEADS), 0.5),
        "norm_mix_w": 1.0 + nrm(ks[7], (DEPTH, D_MODEL), 0.02),
        "w_in": nrm(ks[8], (DEPTH, D_MODEL, N_IN), D_MODEL ** -0.5),
        "conv_w": nrm(ks[9], (DEPTH, SSD_CONV, SSD_CONV_DIM), SSD_CONV ** -0.5),
        "conv_b": nrm(ks[14], (DEPTH, SSD_CONV_DIM), 0.02),
        "dt_bias": dt0 + jnp.log(-jnp.expm1(-dt0)),
        "a_log": jnp.log(jax.random.uniform(ks[11], (DEPTH, SSD_HEADS), f32, 1.0, 16.0)),
        "d_skip": 1.0 + nrm(ks[15], (DEPTH, SSD_HEADS), 0.02),
        "ssd_norm_w": 1.0 + nrm(ks[16], (DEPTH, D_SSD), 0.02),
        "if_bias": if_bias,
        "mlstm_norm_w": 1.0 + nrm(ks[17], (DEPTH, D_MLSTM), 0.02),
        "w_out": nrm(ks[18], (DEPTH, D_INNER, D_MODEL), D_INNER ** -0.5),
        "norm_ffn_w": 1.0 + nrm(ks[19], (DEPTH, D_MODEL), 0.02),
        "peer_w_q": nrm(ks[20], (DEPTH, D_MODEL, PEER_HEADS * PEER_KEY_DIM), D_MODEL ** -0.5),
        "peer_sub_keys": nrm(ks[21], (DEPTH, PEER_HEADS, 2, PEER_KEYS, PEER_HALF), PEER_HALF ** -0.5),
        "peer_u": nrm(ks[22], (DEPTH, PEER_EXPERTS, D_MODEL), D_MODEL ** -0.5),
        "peer_v": nrm(ks[23], (DEPTH, PEER_EXPERTS, D_MODEL), 0.3),
        "final_norm_w": 1.0 + nrm(ks[24], (D_MODEL,), 0.02),
    }


def reference(x_prompt, x_sample, state_ssd_h, state_ssd_conv, state_mlstm_c, state_mlstm_n,
              state_mlstm_m, norm_mix_w, w_in, conv_w, conv_b, dt_bias, a_log, d_skip, ssd_norm_w,
              if_bias, mlstm_norm_w, w_out, norm_ffn_w, peer_w_q, peer_sub_keys, peer_u, peer_v,
              final_norm_w):
    pdt = x_prompt.dtype
    nb = x_prompt.shape[0]
    p_h0 = jnp.zeros((nb, SSD_HEADS, SSD_HEAD_DIM, SSD_STATE), pdt)
    p_conv0 = jnp.zeros((nb, SSD_CONV - 1, SSD_CONV_DIM), pdt)
    p_c0 = jnp.zeros((nb, MLSTM_HEADS, MLSTM_HEAD_DIM, MLSTM_HEAD_DIM), pdt)
    p_n0 = jnp.zeros((nb, MLSTM_HEADS, MLSTM_HEAD_DIM), pdt)
    p_m0 = jnp.zeros((nb, MLSTM_HEADS), pdt)

    xp, xs = x_prompt, x_sample
    p_states = [[], [], [], [], []]
    s_states = [[], [], [], [], []]
    for l in range(DEPTH):
        w = (norm_mix_w[l], w_in[l], conv_w[l], conv_b[l], dt_bias[l], a_log[l], d_skip[l],
             ssd_norm_w[l], if_bias[l], mlstm_norm_w[l], w_out[l], norm_ffn_w[l], peer_w_q[l],
             peer_sub_keys[l], peer_u[l], peer_v[l])
        xp, sp = layer(xp, p_h0, p_conv0, p_c0, p_n0, p_m0, *w)
        xs, ss = layer(xs, state_ssd_h[l], state_ssd_conv[l], state_mlstm_c[l], state_mlstm_n[l],
                       state_mlstm_m[l], *w)
        for i in range(5):
            p_states[i].append(sp[i])
            s_states[i].append(ss[i])
    y_prompt = rms_norm(xp, final_norm_w)
    y_sample = rms_norm(xs, final_norm_w)
    p_ssd_h, p_ssd_conv, p_c, p_n, p_m = [jnp.stack(s, axis=0) for s in p_states]
    s_ssd_h, s_ssd_conv, s_c, s_n, s_m = [jnp.stack(s, axis=0) for s in s_states]
    return (y_prompt, y_sample, p_ssd_h, p_ssd_conv, p_c, p_n, p_m, s_ssd_h, s_ssd_conv, s_c, s_n, s_m)
```

```python
import functools
import math

import jax
import jax.numpy as jnp
from jax import lax
from jax.experimental import pallas as pl
from jax.experimental.pallas import tpu as pltpu

F32 = jnp.float32
BF16 = jnp.bfloat16
HIGHEST = lax.Precision.HIGHEST

RMS_EPS = 1e-6
LANES = 128
VMEM_LIMIT_BYTES = 56 * 1024 * 1024

SSD_HEAD_DIM = 64
SSD_GROUPS = 4
SSD_CONV = 4
MLSTM_HEADS = 4
PEER_HEADS = 8
PEER_TOPK = 16

MIX_ROWS = 256
PEER_EXPERT_BLOCK = 1024
SMALL_F = 16

NT_DIMS = (((1,), (1,)), ((), ()))
TN_DIMS = (((0,), (0,)), ((), ()))


def _rms(x, w):
    return x * lax.rsqrt(jnp.mean(x * x, axis=-1, keepdims=True) + RMS_EPS) * w


def _sigmoid(x):
    return 0.5 * jnp.tanh(0.5 * x) + 0.5


def _split_bf16(x, terms):
    parts = []
    for _ in range(terms):
        p = x.astype(BF16)
        parts.append(p)
        x = x - p.astype(F32)
    return parts


def _expand(x, e2):
    return jnp.dot(jnp.concatenate(_split_bf16(x, 2), axis=1), e2, preferred_element_type=F32)


def _mixer_body(nb, d_ssd, d_ml, n_state,
                x_ref, nw_ref, win_ref, cw_ref, cb_ref, sp_ref, dsk_ref, snw_ref, mnw_ref,
                e16_ref, e16t_ref, e4_ref, wout_ref,
                s0_ref, cv0_ref, c0_ref, n0_ref, m0_ref,
                x1_ref, s_ref, cv_ref, c_ref, n_ref, m_ref,
                p_ref, cvbuf, xbc_s, y_s, ml_s, w_s):
    rows = x_ref.shape[0]
    p_ref[...] = jnp.dot(_rms(x_ref[...], nw_ref[...]).astype(BF16), win_ref[...], preferred_element_type=F32)
    q = rows // nb
    gn = SSD_GROUPS * n_state
    d_conv = d_ssd + 2 * gn
    hpg = d_ssd // SSD_HEAD_DIM // SSD_GROUPS
    gw = hpg * SSD_HEAD_DIM
    hd = d_ml // MLSTM_HEADS
    off_xbc = d_ssd
    off_q = off_xbc + d_conv
    off_k, off_v, off_o = off_q + d_ml, off_q + 2 * d_ml, off_q + 3 * d_ml
    off_s1 = off_q + 4 * d_ml
    off_s2 = off_s1 + LANES

    @pl.when(pl.program_id(1) == 0)
    def _():
        first = pl.program_id(0) * nb
        for src, dst in ((s0_ref, s_ref), (cv0_ref, cv_ref), (c0_ref, c_ref), (n0_ref, n_ref), (m0_ref, m_ref)):
            pltpu.sync_copy(src.at[pl.ds(first, nb)], dst)

    ri = lax.broadcasted_iota(jnp.int32, (rows, rows), 0)
    ci = lax.broadcasted_iota(jnp.int32, (rows, rows), 1)
    if nb > 1:
        shift = int(math.log2(q))
        same = (ri >> shift) == (ci >> shift)
        mask = (ri >= ci) & same
        mask_t = (ci >= ri) & same
    else:
        mask = ri >= ci
        mask_t = ci >= ri
    m_tril = mask.astype(F32)
    m_tril_t = mask_t.astype(F32)
    seq_rows = [slice(b * q, (b + 1) * q) for b in range(nb)]
    row_id = lax.broadcasted_iota(jnp.int32, (rows, 1), 0)

    def seq_mask(b, v):
        if nb == 1:
            return v
        return jnp.where((row_id >= b * q) & (row_id < (b + 1) * q), v, 0.0)

    lane = lax.broadcasted_iota(jnp.int32, (1, LANES), 1)
    n_ssd_heads = d_ssd // SSD_HEAD_DIM
    sp = sp_ref[...]
    a_row = jnp.where(lane < n_ssd_heads, -jnp.exp(sp[2:3, :]), 0.0)
    pre1 = p_ref[:, off_s1:off_s1 + LANES] + sp[0:1, :]
    pre2 = p_ref[:, off_s2:off_s2 + LANES] + sp[1:2, :]
    dt = jax.nn.softplus(pre1)
    lf = jax.nn.log_sigmoid(pre1)
    cs_in = jnp.where(lane < n_ssd_heads, dt * a_row,
                      jnp.where(lane < SMALL_F + MLSTM_HEADS, lf, 0.0))
    cum = sum(jnp.dot(m_tril.astype(BF16), p, preferred_element_type=F32) for p in _split_bf16(cs_in, 3))
    if nb > 1:
        tot = jnp.dot(same.astype(F32), cs_in, precision=HIGHEST, preferred_element_type=F32)
    else:
        tot = jnp.broadcast_to(cum[rows - 1:rows, :], (rows, LANES))
    cum_t = sum(jnp.dot(p, m_tril_t.astype(BF16), preferred_element_type=F32) for p in _split_bf16(cs_in.T, 3))
    ig_t = pre2.T

    for b in range(nb):
        rb = seq_rows[b]
        cvbuf[0:8, :] = cv_ref[b]
        cvbuf[8:8 + q, :] = p_ref[rb, off_xbc:off_xbc + d_conv]
        acc = cb_ref[...] + cw_ref[0:1, :] * cvbuf[pl.ds(8 - (SSD_CONV - 1), q), :]
        for k in range(1, SSD_CONV):
            acc = acc + cw_ref[k:k + 1, :] * cvbuf[pl.ds(8 - (SSD_CONV - 1) + k, q), :]
        xbc_s[rb, :] = acc * _sigmoid(acc)
        cv_ref[b] = cvbuf[q:q + 8, :]

    e16 = e16_ref[...]
    xs = xbc_s[:, 0:d_ssd]
    dt_full = _expand(dt, e16)
    dec_full = _expand(jnp.exp(cum), e16)
    te_full = _expand(jnp.exp(tot - cum), e16)
    xdt = xs * dt_full
    xw = xdt * te_full
    lane_lo = lax.broadcasted_iota(jnp.int32, (1, 2 * SSD_HEAD_DIM), 1) < SSD_HEAD_DIM
    for g in range(SSD_GROUPS):
        bg = xbc_s[:, d_ssd + g * n_state:d_ssd + (g + 1) * n_state].astype(BF16)
        cg = xbc_s[:, d_ssd + gn + g * n_state:d_ssd + gn + (g + 1) * n_state].astype(BF16)
        cb = lax.dot_general(cg, bg, NT_DIMS, preferred_element_type=F32)
        for pr in range(hpg // 2):
            c0 = g * gw + pr * 2 * SSD_HEAD_DIM
            x2 = xdt[:, c0:c0 + 2 * SSD_HEAD_DIM].astype(BF16)
            acc = None
            for half in range(2):
                h = g * hpg + pr * 2 + half
                diff = (jnp.broadcast_to(cum[:, h:h + 1], (rows, rows))
                        - jnp.broadcast_to(cum_t[h:h + 1, :], (rows, rows)))
                seg = jnp.exp(jnp.where(mask, diff, -jnp.inf))
                a_h = (cb * seg).astype(BF16)
                xm = jnp.where(lane_lo if half == 0 else jnp.logical_not(lane_lo), x2, jnp.zeros_like(x2))
                part = jnp.dot(a_h, xm, preferred_element_type=F32)
                acc = part if acc is None else acc + part
            y_s[:, c0:c0 + 2 * SSD_HEAD_DIM] = acc
        gs = slice(g * gw, (g + 1) * gw)
        for b in range(nb):
            rb = seq_rows[b]
            s_old = s_ref[b, gs, :]
            yi = lax.dot_general(cg[rb], s_old.astype(BF16), NT_DIMS, preferred_element_type=F32)
            y_s[rb, gs] = y_s[rb, gs] + yi * dec_full[rb, gs]
            e = (b + 1) * q - 1
            tot_b = jnp.broadcast_to(cum_t[:, e:e + 1], (LANES, LANES))
            dec_s = jnp.exp(jnp.dot(e16t_ref[gs, :], tot_b, precision=HIGHEST, preferred_element_type=F32))
            xwb = seq_mask(b, xw[:, gs]).astype(BF16)
            s_ref[b, gs, :] = dec_s * s_old + lax.dot_general(xwb, bg, TN_DIMS, preferred_element_type=F32)
    y = y_s[...] + dsk_ref[...] * xs
    z = p_ref[:, 0:d_ssd]
    y = y * (z * _sigmoid(z))
    for g in range(SSD_GROUPS):
        gs = slice(g * gw, (g + 1) * gw)
        yg = y[:, gs]
        y_s[:, gs] = yg * lax.rsqrt(jnp.mean(yg * yg, axis=-1, keepdims=True) + RMS_EPS) * snw_ref[:, gs]

    e4 = e4_ref[...]
    if nb == 1:
        m_rows = jnp.broadcast_to(m_ref[0:1, :], (rows, LANES))
    else:
        m_rows = jnp.concatenate([jnp.broadcast_to(m_ref[b:b + 1, :], (q, LANES)) for b in range(nb)], axis=0)
    g_all = tot - cum + pre2
    inter_all = cum + m_rows
    m_new, dec_rows = [], []
    for b in range(nb):
        rb = seq_rows[b]
        b_end_m = tot[b * q:b * q + 1, :] + m_ref[b:b + 1, :]
        mn = jnp.maximum(b_end_m, jnp.max(g_all[rb], axis=0, keepdims=True))
        w_s[rb, :] = jnp.exp(g_all[rb] - mn)
        dec = jnp.exp(b_end_m - mn)
        dec_rows.append(jnp.dot(jnp.broadcast_to(dec, (8, LANES)), e4[0:LANES, :].astype(F32), precision=HIGHEST,
                                preferred_element_type=F32)[0:1, :])
        m_new.append(mn)
    w_full = _expand(w_s[...], e4)
    k_scale = hd ** -0.5
    kw = p_ref[:, off_k:off_k + d_ml] * k_scale * w_full
    for h in range(MLSTM_HEADS):
        hs = slice(h * hd, (h + 1) * hd)
        col = SMALL_F + h
        qh = p_ref[:, off_q + h * hd:off_q + (h + 1) * hd]
        qb = qh.astype(BF16)
        kb = (p_ref[:, off_k + h * hd:off_k + (h + 1) * hd] * k_scale).astype(BF16)
        vb = p_ref[:, off_v + h * hd:off_v + (h + 1) * hd].astype(BF16)
        d = (jnp.broadcast_to(cum[:, col:col + 1], (rows, rows))
             - jnp.broadcast_to(cum_t[col:col + 1, :], (rows, rows))
             + jnp.broadcast_to(ig_t[col:col + 1, :], (rows, rows)))
        d = jnp.where(mask, d, -jnp.inf)
        inter = inter_all[:, col:col + 1]
        m_t = jnp.maximum(inter, jnp.max(d, axis=-1, keepdims=True))
        w_intra = jnp.exp(d - m_t)
        w_inter = jnp.exp(inter - m_t)
        s = lax.dot_general(qb, kb, NT_DIMS, preferred_element_type=F32) * w_intra
        num = jnp.dot(s.astype(BF16), vb, preferred_element_type=F32)
        qc = [jnp.dot(qb[seq_rows[b]], c_ref[b, hs, :].astype(BF16), preferred_element_type=F32)
              for b in range(nb)]
        nf = [jnp.broadcast_to(n_ref[b:b + 1, hs], (q, hd)) for b in range(nb)]
        qc = qc[0] if nb == 1 else jnp.concatenate(qc, axis=0)
        nf = nf[0] if nb == 1 else jnp.concatenate(nf, axis=0)
        qn = jnp.sum(qh * nf, axis=-1, keepdims=True)
        num = num + qc * w_inter
        den = jnp.sum(s, axis=-1, keepdims=True) + w_inter * qn
        den = jnp.maximum(jnp.abs(den), jnp.exp(-m_t))
        hh = num / den
        hh = hh * lax.rsqrt(jnp.mean(hh * hh, axis=-1, keepdims=True) + RMS_EPS) * mnw_ref[:, hs]
        o = p_ref[:, off_o + h * hd:off_o + (h + 1) * hd]
        ml_s[:, hs] = hh * _sigmoid(o)
        for b in range(nb):
            kwb = seq_mask(b, kw[:, hs]).astype(BF16)
            c_ref[b, hs, :] = (jnp.broadcast_to(dec_rows[b][:, hs], (hd, hd)) * c_ref[b, hs, :]
                               + lax.dot_general(kwb, vb, TN_DIMS, preferred_element_type=F32))
    for b in range(nb):
        n_ref[b:b + 1, :] = dec_rows[b] * n_ref[b:b + 1, :] + jnp.sum(kw[seq_rows[b]], axis=0, keepdims=True)
        m_ref[b:b + 1, :] = m_new[b]

    mix = jnp.concatenate([y_s[...].astype(BF16), ml_s[...].astype(BF16)], axis=1)
    x1_ref[...] = x_ref[...] + jnp.dot(mix, wout_ref[...], preferred_element_type=F32)


def _mixer(x2d, nw, w_perm, bsz, seq, cw, cb, sp, dsk, snw, mnw, e16, e16t, e4, wout, s0, cv0, c0, n0, m0):
    t, d = x2d.shape
    n_cols = w_perm.shape[1]
    d_ssd = dsk.shape[1]
    d_ml = mnw.shape[1]
    n_state = s0.shape[2]
    d_conv = cw.shape[1]
    q = min(seq, MIX_ROWS)
    nb = MIX_ROWS // q
    assert seq % q == 0 and bsz % nb == 0 and (nb == 1 or q == seq)
    n_chunks = seq // q
    const = lambda shape: pl.BlockSpec(shape, lambda bi, ci: (0,) * len(shape), pipeline_mode=pl.Buffered(1))
    tok = lambda width: pl.BlockSpec((MIX_ROWS, width), lambda bi, ci: (bi * n_chunks + ci, 0))
    st3 = lambda a: pl.BlockSpec((nb,) + a.shape[1:], lambda bi, ci: (bi, 0, 0))
    st2 = lambda a: pl.BlockSpec((nb,) + a.shape[1:], lambda bi, ci: (bi, 0))
    hbm = pl.BlockSpec(memory_space=pl.ANY)
    body = functools.partial(_mixer_body, nb, d_ssd, d_ml, n_state)
    return pl.pallas_call(
        body,
        grid=(bsz // nb, n_chunks),
        in_specs=[tok(d), const(nw.shape), const(w_perm.shape), const(cw.shape), const(cb.shape), const(sp.shape),
                  const(dsk.shape),
                  const(snw.shape), const(mnw.shape), const(e16.shape), const(e16t.shape), const(e4.shape),
                  const(wout.shape), hbm, hbm, hbm, hbm, hbm],
        out_specs=[tok(d), st3(s0), st3(cv0), st3(c0), st2(n0), st2(m0)],
        out_shape=[jax.ShapeDtypeStruct((t, d), F32)] + [jax.ShapeDtypeStruct(a.shape, F32) for a in (s0, cv0, c0, n0, m0)],
        scratch_shapes=[
            pltpu.VMEM((MIX_ROWS, n_cols), F32),
            pltpu.VMEM((8 + q, d_conv), F32),
            pltpu.VMEM((MIX_ROWS, d_conv), F32),
            pltpu.VMEM((MIX_ROWS, d_ssd), F32),
            pltpu.VMEM((MIX_ROWS, d_ml), F32),
            pltpu.VMEM((MIX_ROWS, LANES), F32),
        ],
        compiler_params=pltpu.CompilerParams(
            dimension_semantics=("arbitrary", "arbitrary"), vmem_limit_bytes=VMEM_LIMIT_BYTES),
        name="mixer",
    )(x2d, nw, w_perm, cw, cb, sp, dsk, snw, mnw, e16, e16t, e4, wout, s0, cv0, c0, n0, m0)


def _sort_network(n):
    pairs = []
    p = 1
    while p < n:
        k = p
        while k >= 1:
            for j in range(k % p, n - k, 2 * k):
                for i in range(min(k, n - j - k)):
                    if (i + j) // (2 * p) == (i + j + k) // (2 * p):
                        pairs.append((i + j, i + j + k))
            k //= 2
        p *= 2
    return pairs


def _compare_exchange(v, i, j):
    v[i], v[j] = jnp.maximum(v[i], v[j]), jnp.minimum(v[i], v[j])


def _bitonic_merge(v):
    v = list(v)
    d = len(v) // 2
    while d >= 1:
        for i in range(len(v)):
            if (i & d) == 0:
                _compare_exchange(v, i, i + d)
        d //= 2
    return v


def _merge_top(top, other):
    n = len(top)
    c = list(top)
    for k in range(n - len(other), n):
        c[k] = jnp.maximum(top[k], other[n - 1 - k])
    return _bitonic_merge(c)


def _column_top(vregs, k):
    v = list(vregs)
    for i, j in _sort_network(k):
        _compare_exchange(v, i, j)
    for shift in (4, 2, 1):
        v = _merge_top(v, [pltpu.roll(x, shift, 0) for x in v])
    return v


def _peer_body(n_keys, x_ref, nw_ref, wq_ref, keys_ref, u0_ref, ua_ref, ub_ref, vta_ref, vtb_ref, vtl_ref,
               fnw_ref, y_ref, xh_s, qt_s, top_s, res_s, cnt_s, p1_s, rank_s, p2_s,
               at0_s, at1_s, act0_s, act1_s, acc_s):
    j = pl.program_id(1)
    tm = x_ref.shape[0]
    eb = ua_ref.shape[0]
    n1_per_blk = eb // n_keys
    pack = 16

    @pl.when(j == 0)
    def _():
        xh = _rms(x_ref[...], nw_ref[...]).astype(BF16)
        xh_s[...] = xh
        qt_s[...] = lax.dot_general(wq_ref[...], xh, NT_DIMS, preferred_element_type=F32).astype(BF16)
        half = keys_ref.shape[2]
        topk = PEER_TOPK
        lane_tiles = [slice(lt * LANES, (lt + 1) * LANES) for lt in range(tm // LANES)]
        width = lambda i: topk // (i + 1)

        def scores(h, c):
            r0 = (h * 2 + c) * half
            return jnp.dot(keys_ref[h * 2 + c], qt_s[r0:r0 + half, :], preferred_element_type=F32)

        def tiles(sc, ls):
            return [sc[k * 8:(k + 1) * 8, ls] for k in range(n_keys // 8)]

        for h in range(PEER_HEADS):
            for c in range(2):
                sc = scores(h, c)
                for ls in lane_tiles:
                    top = _column_top(tiles(sc, ls), topk)
                    for k in range(topk):
                        top_s[c, k, h:h + 1, ls] = top[k][0:1, :]

        for ls in lane_tiles:
            a1 = [top_s[0, k, :, ls] for k in range(topk)]
            a2 = [top_s[1, k, :, ls] for k in range(topk)]
            top = [a1[i] + a2[0] for i in range(topk)]
            for jc in range(1, topk // 2):
                top = _merge_top(top, [a1[i] + a2[jc] for i in range(width(jc))])
            top = _merge_top(top, [a1[0] + a2[jc] for jc in range(topk // 2, topk)])
            tau = top[topk - 1]
            s_max = a1[0] + a2[0]
            z = jnp.zeros_like(tau)
            for i in range(topk):
                cnt_i = jnp.zeros_like(tau)
                for jc in range(width(i)):
                    cand = a1[i] + a2[jc]
                    sel = cand >= tau
                    cnt_i = cnt_i + jnp.where(sel, 1.0, 0.0)
                    z = z + jnp.where(sel, jnp.exp(cand - s_max), 0.0)
                res_s[i, :, ls] = cnt_i
            res_s[topk, :, ls] = 1.0 / z

        for h in range(PEER_HEADS):
            sc1, sc2 = scores(h, 0), scores(h, 1)
            for lt, ls in enumerate(lane_tiles):
                rep = lambda row: jnp.broadcast_to(row, (8, LANES))
                a1 = [rep(top_s[0, k, h:h + 1, ls]) for k in range(topk)]
                cnt_i = [rep(res_s[i, h:h + 1, ls]) for i in range(topk)]
                inv_z = rep(res_s[topk, h:h + 1, ls])
                for r, s1v in enumerate(tiles(sc1, ls)):
                    cnt = jnp.zeros_like(s1v)
                    for i in range(topk):
                        cnt = jnp.where(s1v == a1[i], cnt_i[i], cnt)
                    cnt_s[h, lt, r * 8:(r + 1) * 8, :] = cnt
                    p1_s[h, lt, r * 8:(r + 1) * 8, :] = jnp.exp(s1v - a1[0]) * inv_z
                a2 = [rep(top_s[1, k, h:h + 1, ls]) for k in range(topk)]
                s2v = tiles(sc2, ls)
                for r in range(0, len(s2v), 2):
                    rank, p2 = [], []
                    for v in s2v[r:r + 2]:
                        rk = jnp.full_like(v, float(topk))
                        for k in reversed(range(topk)):
                            rk = jnp.where(v >= a2[k], float(k), rk)
                        rank.append(rk)
                        p2.append(jnp.exp(v - a2[0]))
                    rank_s[h, r * 8:(r + 2) * 8, ls] = jnp.concatenate(rank, axis=0).astype(BF16)
                    p2_s[h, r * 8:(r + 2) * 8, ls] = jnp.concatenate(p2, axis=0).astype(BF16)
        acc_s[...] = jnp.zeros_like(acc_s)
        act1_s[...] = jnp.zeros_like(act1_s)
        at0_s[...] = lax.dot_general(u0_ref[...], xh, NT_DIMS, preferred_element_type=F32).astype(BF16)

    def packed_row(ref, h, n1):
        tiles_ = [ref[h, lt, pl.ds(n1, pack, stride=0), :] for lt in range(tm // LANES)]
        return jnp.concatenate(tiles_, axis=1).astype(BF16)

    def gated_act(blk, at_ref, act_ref):
        for i in range(n1_per_blk):
            n1 = blk * n1_per_blk + i
            cnt_b = [packed_row(cnt_s, h, n1) for h in range(PEER_HEADS)]
            p1_b = [packed_row(p1_s, h, n1) for h in range(PEER_HEADS)]
            for r in range(n_keys // pack):
                rs = slice(r * pack, (r + 1) * pack)
                gate = None
                for h in range(PEER_HEADS):
                    p2 = p2_s[h, rs, :]
                    term = jnp.where(rank_s[h, rs, :] < cnt_b[h], p2, jnp.zeros_like(p2)) * p1_b[h]
                    gate = term if gate is None else gate + term
                es = slice(i * n_keys + r * pack, i * n_keys + (r + 1) * pack)
                a = at_ref[es, :]
                gelu = a * (lax.erf(a * (2.0 ** -0.5)) * 0.5 + 0.5)
                act_ref[es, :] = gelu * gate

    acc_s[...] += jnp.dot(vta_ref[...], act1_s[...], preferred_element_type=F32)
    at1_s[...] = lax.dot_general(ua_ref[...], xh_s[...], NT_DIMS, preferred_element_type=F32).astype(BF16)
    gated_act(2 * j, at0_s, act0_s)
    acc_s[...] += jnp.dot(vtb_ref[...], act0_s[...], preferred_element_type=F32)
    at0_s[...] = lax.dot_general(ub_ref[...], xh_s[...], NT_DIMS, preferred_element_type=F32).astype(BF16)
    gated_act(2 * j + 1, at1_s, act1_s)

    @pl.when(j == pl.num_programs(1) - 1)
    def _():
        acc = acc_s[...] + jnp.dot(vtl_ref[...], act1_s[...], preferred_element_type=F32)
        y_ref[...] = _rms(x_ref[...] + acc.T, fnw_ref[...])


def _peer(x1, nw, wq_t, keys, u_b, vt_b, fnw):
    t, d = x1.shape
    n_exp = u_b.shape[0]
    n_keys = keys.shape[1]
    tm = 512 if t % 512 == 0 else MIX_ROWS
    eb = PEER_EXPERT_BLOCK
    n_blk = n_exp // eb
    assert t % tm == 0 and n_exp % (2 * eb) == 0 and eb % n_keys == 0 and n_exp == n_keys * n_keys
    assert n_keys == 8 * PEER_TOPK and PEER_HEADS == 8
    once = pl.Buffered(1)
    const = lambda shape: pl.BlockSpec(shape, lambda i, j: (0,) * len(shape), pipeline_mode=once)
    head_f32 = pltpu.VMEM((PEER_HEADS, tm // LANES, n_keys, LANES), F32)
    head_bf16 = pltpu.VMEM((PEER_HEADS, n_keys, tm), BF16)
    return pl.pallas_call(
        functools.partial(_peer_body, n_keys),
        grid=(t // tm, n_blk // 2),
        in_specs=[
            pl.BlockSpec((tm, d), lambda i, j: (i, 0)),
            const(nw.shape), const(wq_t.shape), const(keys.shape),
            pl.BlockSpec((eb, d), lambda i, j: (0, 0), pipeline_mode=once),
            pl.BlockSpec((eb, d), lambda i, j: (2 * j + 1, 0)),
            pl.BlockSpec((eb, d), lambda i, j: (jnp.minimum(2 * j + 2, n_blk - 1), 0)),
            pl.BlockSpec((d, eb), lambda i, j: (0, jnp.maximum(2 * j - 1, 0))),
            pl.BlockSpec((d, eb), lambda i, j: (0, 2 * j)),
            pl.BlockSpec((d, eb), lambda i, j: (0, n_blk - 1), pipeline_mode=once),
            const(fnw.shape),
        ],
        out_specs=pl.BlockSpec((tm, d), lambda i, j: (i, 0)),
        out_shape=jax.ShapeDtypeStruct((t, d), F32),
        scratch_shapes=[
            pltpu.VMEM((tm, d), BF16),
            pltpu.VMEM((wq_t.shape[0], tm), BF16),
            pltpu.VMEM((2, PEER_TOPK, PEER_HEADS, tm), F32),
            pltpu.VMEM((PEER_TOPK + 1, PEER_HEADS, tm), F32),
            head_f32, head_f32, head_bf16, head_bf16,
            pltpu.VMEM((eb, tm), BF16), pltpu.VMEM((eb, tm), BF16),
            pltpu.VMEM((eb, tm), BF16), pltpu.VMEM((eb, tm), BF16),
            pltpu.VMEM((d, tm), F32),
        ],
        compiler_params=pltpu.CompilerParams(
            dimension_semantics=("arbitrary", "arbitrary"), vmem_limit_bytes=VMEM_LIMIT_BYTES),
        name="peer",
    )(x1, nw, wq_t, keys, u_b, u_b, u_b, vt_b, vt_b, vt_b, fnw)


def _expander(first_row, n_heads, width):
    r = lax.broadcasted_iota(jnp.int32, (LANES, n_heads * width), 0)
    c = lax.broadcasted_iota(jnp.int32, (LANES, n_heads * width), 1)
    return (r == first_row + c // width).astype(F32)


def _pad_lanes(v, first):
    return jnp.zeros((v.shape[0], LANES), F32).at[:, first:first + v.shape[1]].set(v)


def kernel(x_prompt, x_sample, state_ssd_h, state_ssd_conv, state_mlstm_c, state_mlstm_n, state_mlstm_m,
           norm_mix_w, w_in, conv_w, conv_b, dt_bias, a_log, d_skip, ssd_norm_w, if_bias, mlstm_norm_w,
           w_out, norm_ffn_w, peer_w_q, peer_sub_keys, peer_u, peer_v, final_norm_w):
    depth = w_in.shape[0]
    assert depth == 1, "the PEER kernel fuses the final norm, so it must be the last layer's"
    d_model = x_prompt.shape[-1]
    n_ssd_heads = dt_bias.shape[1]
    d_ssd = ssd_norm_w.shape[1]
    d_ml = mlstm_norm_w.shape[1]
    d_conv = conv_w.shape[2]
    n_state = state_ssd_h.shape[-1]
    hd = d_ml // MLSTM_HEADS
    off_dt = d_ssd + d_conv
    off_q = off_dt + n_ssd_heads
    off_if = off_q + 4 * d_ml
    assert n_ssd_heads <= SMALL_F and d_ssd == n_ssd_heads * SSD_HEAD_DIM

    e16t = _expander(0, n_ssd_heads, SSD_HEAD_DIM).T
    twice = lambda e: jnp.concatenate([e, e], axis=0).astype(BF16)
    e16 = twice(_expander(0, n_ssd_heads, SSD_HEAD_DIM))
    e4 = twice(_expander(SMALL_F, MLSTM_HEADS, hd))

    groups = []
    for x, states in ((x_prompt, None),
                      (x_sample, (state_ssd_h, state_ssd_conv, state_mlstm_c, state_mlstm_n, state_mlstm_m))):
        bsz, seq, _ = x.shape
        if states is None:
            st = [(jnp.zeros((bsz, d_ssd, n_state), F32), jnp.zeros((bsz, 8, d_conv), F32),
                   jnp.zeros((bsz, d_ml, hd), F32), jnp.zeros((bsz, d_ml), F32),
                   jnp.zeros((bsz, LANES), F32)) for _ in range(depth)]
        else:
            h0, cv0, c0, n0, m0 = states
            st = [(h0[l].reshape(bsz, d_ssd, n_state),
                   jnp.pad(cv0[l], ((0, 0), (8 - (SSD_CONV - 1), 0), (0, 0))),
                   c0[l].reshape(bsz, d_ml, hd), n0[l].reshape(bsz, d_ml),
                   _pad_lanes(m0[l], SMALL_F)) for l in range(depth)]
        groups.append([x.reshape(bsz * seq, d_model), bsz, seq, st, []])

    for l in range(depth):
        w = w_in[l].astype(BF16)
        zeros = lambda n: jnp.zeros((d_model, n), BF16)
        gates1 = jnp.concatenate([w[:, off_dt:off_q], zeros(SMALL_F - n_ssd_heads),
                                  w[:, off_if + MLSTM_HEADS:off_if + 2 * MLSTM_HEADS],
                                  zeros(LANES - SMALL_F - MLSTM_HEADS)], axis=1)
        gates2 = jnp.concatenate([zeros(SMALL_F), w[:, off_if:off_if + MLSTM_HEADS],
                                  zeros(LANES - SMALL_F - MLSTM_HEADS)], axis=1)
        w_perm = jnp.concatenate([w[:, 0:off_dt], w[:, off_q:off_if], gates1, gates2], axis=1)
        sp = jnp.zeros((8, LANES), F32)
        sp = sp.at[0, 0:n_ssd_heads].set(dt_bias[l]).at[0, SMALL_F:SMALL_F + MLSTM_HEADS].set(if_bias[l, MLSTM_HEADS:])
        sp = sp.at[1, SMALL_F:SMALL_F + MLSTM_HEADS].set(if_bias[l, :MLSTM_HEADS])
        sp = sp.at[2, 0:n_ssd_heads].set(a_log[l])
        dsk = jnp.repeat(d_skip[l], SSD_HEAD_DIM)[None, :]
        wout = w_out[l].astype(BF16)
        wq_t = peer_w_q[l].astype(BF16).T
        keys = peer_sub_keys[l].reshape((-1,) + peer_sub_keys.shape[-2:]).astype(BF16)
        u_b = peer_u[l].astype(BF16)
        vt_b = peer_v[l].astype(BF16).T
        fnw = final_norm_w[None, :]
        for grp in groups:
            x2d, bsz, seq, st, outs = grp
            x1, s_n, cv_n, c_n, n_n, m_n = _mixer(
                x2d, norm_mix_w[l][None, :], w_perm, bsz, seq, conv_w[l], conv_b[l][None, :], sp, dsk, ssd_norm_w[l][None, :],
                mlstm_norm_w[l][None, :], e16, e16t, e4, wout, *st[l])
            outs.append((s_n, cv_n, c_n, n_n, m_n))
            grp[0] = _peer(x1, norm_ffn_w[l][None, :], wq_t, keys, u_b, vt_b, fnw)

    results = []
    for x2d, bsz, seq, st, outs in groups:
        results.append(x2d.reshape(bsz, seq, d_model))
    state_out = []
    for x2d, bsz, seq, st, outs in groups:
        hs = jnp.stack([o[0].reshape(bsz, n_ssd_heads, SSD_HEAD_DIM, n_state) for o in outs])
        cvs = jnp.stack([o[1][:, 8 - (SSD_CONV - 1):, :] for o in outs])
        cs = jnp.stack([o[2].reshape(bsz, MLSTM_HEADS, hd, hd) for o in outs])
        ns = jnp.stack([o[3].reshape(bsz, MLSTM_HEADS, hd) for o in outs])
        ms = jnp.stack([o[4][:, SMALL_F:SMALL_F + MLSTM_HEADS] for o in outs])
        state_out.append((hs, cvs, cs, ns, ms))
    return (results[0], results[1]) + state_out[0] + state_out[1]
```

```python
import functools
import math

import jax
import jax.numpy as jnp
from jax import lax
from jax.experimental import pallas as pl
from jax.experimental.pallas import tpu as pltpu

F32 = jnp.float32
BF16 = jnp.bfloat16
HIGHEST = lax.Precision.HIGHEST

RMS_EPS = 1e-6
LANES = 128
VMEM_LIMIT_BYTES = 56 * 1024 * 1024

SSD_HEAD_DIM = 64
SSD_GROUPS = 4
SSD_CONV = 4
MLSTM_HEADS = 4
PEER_HEADS = 8
PEER_TOPK = 16

MIX_ROWS = 256
PEER_EXPERT_BLOCK = 2048
SMALL_F = 16

NT_DIMS = (((1,), (1,)), ((), ()))
TN_DIMS = (((0,), (0,)), ((), ()))


def _rms(x, w):
    return x * lax.rsqrt(jnp.mean(x * x, axis=-1, keepdims=True) + RMS_EPS) * w


def _sigmoid(x):
    return 0.5 * jnp.tanh(0.5 * x) + 0.5


def _split_bf16(x, terms):
    parts = []
    for _ in range(terms):
        p = x.astype(BF16)
        parts.append(p)
        x = x - p.astype(F32)
    return parts


def _expand(x, e2):
    return jnp.dot(jnp.concatenate(_split_bf16(x, 2), axis=1), e2, preferred_element_type=F32)


def _mixer_body(nb, d_ssd, d_ml, n_state,
                x_ref, nw_ref, win_ref, cw_ref, cb_ref, sp_ref, dsk_ref, snw_ref, mnw_ref,
                e16_ref, e16t_ref, e4_ref, wout_ref,
                s0_ref, cv0_ref, c0_ref, n0_ref, m0_ref,
                x1_ref, s_ref, cv_ref, c_ref, n_ref, m_ref,
                p_ref, cvbuf, xbc_s, y_s, ml_s, w_s):
    rows = x_ref.shape[0]
    p_ref[...] = jnp.dot(_rms(x_ref[...], nw_ref[...]).astype(BF16), win_ref[...], preferred_element_type=F32)
    q = rows // nb
    gn = SSD_GROUPS * n_state
    d_conv = d_ssd + 2 * gn
    hpg = d_ssd // SSD_HEAD_DIM // SSD_GROUPS
    gw = hpg * SSD_HEAD_DIM
    hd = d_ml // MLSTM_HEADS
    off_xbc = d_ssd
    off_q = off_xbc + d_conv
    off_k, off_v, off_o = off_q + d_ml, off_q + 2 * d_ml, off_q + 3 * d_ml
    off_s1 = off_q + 4 * d_ml
    off_s2 = off_s1 + LANES

    @pl.when(pl.program_id(1) == 0)
    def _():
        first = pl.program_id(0) * nb
        for src, dst in ((s0_ref, s_ref), (cv0_ref, cv_ref), (c0_ref, c_ref), (n0_ref, n_ref), (m0_ref, m_ref)):
            pltpu.sync_copy(src.at[pl.ds(first, nb)], dst)

    ri = lax.broadcasted_iota(jnp.int32, (rows, rows), 0)
    ci = lax.broadcasted_iota(jnp.int32, (rows, rows), 1)
    if nb > 1:
        shift = int(math.log2(q))
        same = (ri >> shift) == (ci >> shift)
        mask = (ri >= ci) & same
        mask_t = (ci >= ri) & same
    else:
        mask = ri >= ci
        mask_t = ci >= ri
    m_tril = mask.astype(F32)
    m_tril_t = mask_t.astype(F32)
    seq_rows = [slice(b * q, (b + 1) * q) for b in range(nb)]
    row_id = lax.broadcasted_iota(jnp.int32, (rows, 1), 0)

    def seq_mask(b, v):
        if nb == 1:
            return v
        return jnp.where((row_id >= b * q) & (row_id < (b + 1) * q), v, 0.0)

    lane = lax.broadcasted_iota(jnp.int32, (1, LANES), 1)
    n_ssd_heads = d_ssd // SSD_HEAD_DIM
    sp = sp_ref[...]
    a_row = jnp.where(lane < n_ssd_heads, -jnp.exp(sp[2:3, :]), 0.0)
    pre1 = p_ref[:, off_s1:off_s1 + LANES] + sp[0:1, :]
    pre2 = p_ref[:, off_s2:off_s2 + LANES] + sp[1:2, :]
    dt = jax.nn.softplus(pre1)
    lf = jax.nn.log_sigmoid(pre1)
    cs_in = jnp.where(lane < n_ssd_heads, dt * a_row,
                      jnp.where(lane < SMALL_F + MLSTM_HEADS, lf, 0.0))
    cum = sum(jnp.dot(m_tril.astype(BF16), p, preferred_element_type=F32) for p in _split_bf16(cs_in, 3))
    if nb > 1:
        tot = jnp.dot(same.astype(F32), cs_in, precision=HIGHEST, preferred_element_type=F32)
    else:
        tot = jnp.broadcast_to(cum[rows - 1:rows, :], (rows, LANES))
    cum_t = sum(jnp.dot(p, m_tril_t.astype(BF16), preferred_element_type=F32) for p in _split_bf16(cs_in.T, 3))
    ig_t = pre2.T

    for b in range(nb):
        rb = seq_rows[b]
        cvbuf[0:8, :] = cv_ref[b]
        cvbuf[8:8 + q, :] = p_ref[rb, off_xbc:off_xbc + d_conv]
        acc = cb_ref[...] + cw_ref[0:1, :] * cvbuf[pl.ds(8 - (SSD_CONV - 1), q), :]
        for k in range(1, SSD_CONV):
            acc = acc + cw_ref[k:k + 1, :] * cvbuf[pl.ds(8 - (SSD_CONV - 1) + k, q), :]
        xbc_s[rb, :] = acc * _sigmoid(acc)
        cv_ref[b] = cvbuf[q:q + 8, :]

    e16 = e16_ref[...]
    xs = xbc_s[:, 0:d_ssd]
    dt_full = _expand(dt, e16)
    dec_full = _expand(jnp.exp(cum), e16)
    te_full = _expand(jnp.exp(tot - cum), e16)
    xdt = xs * dt_full
    xw = xdt * te_full
    lane_lo = lax.broadcasted_iota(jnp.int32, (1, 2 * SSD_HEAD_DIM), 1) < SSD_HEAD_DIM
    for g in range(SSD_GROUPS):
        bg = xbc_s[:, d_ssd + g * n_state:d_ssd + (g + 1) * n_state].astype(BF16)
        cg = xbc_s[:, d_ssd + gn + g * n_state:d_ssd + gn + (g + 1) * n_state].astype(BF16)
        cb = lax.dot_general(cg, bg, NT_DIMS, preferred_element_type=F32)
        for pr in range(hpg // 2):
            c0 = g * gw + pr * 2 * SSD_HEAD_DIM
            x2 = xdt[:, c0:c0 + 2 * SSD_HEAD_DIM].astype(BF16)
            acc = None
            for half in range(2):
                h = g * hpg + pr * 2 + half
                diff = (jnp.broadcast_to(cum[:, h:h + 1], (rows, rows))
                        - jnp.broadcast_to(cum_t[h:h + 1, :], (rows, rows)))
                seg = jnp.exp(jnp.where(mask, diff, -jnp.inf))
                a_h = (cb * seg).astype(BF16)
                xm = jnp.where(lane_lo if half == 0 else jnp.logical_not(lane_lo), x2, jnp.zeros_like(x2))
                part = jnp.dot(a_h, xm, preferred_element_type=F32)
                acc = part if acc is None else acc + part
            y_s[:, c0:c0 + 2 * SSD_HEAD_DIM] = acc
        gs = slice(g * gw, (g + 1) * gw)
        for b in range(nb):
            rb = seq_rows[b]
            s_old = s_ref[b, gs, :]
            yi = lax.dot_general(cg[rb], s_old.astype(BF16), NT_DIMS, preferred_element_type=F32)
            y_s[rb, gs] = y_s[rb, gs] + yi * dec_full[rb, gs]
            e = (b + 1) * q - 1
            tot_b = jnp.broadcast_to(cum_t[:, e:e + 1], (LANES, LANES))
            dec_s = jnp.exp(jnp.dot(e16t_ref[gs, :], tot_b, precision=HIGHEST, preferred_element_type=F32))
            xwb = seq_mask(b, xw[:, gs]).astype(BF16)
            s_ref[b, gs, :] = dec_s * s_old + lax.dot_general(xwb, bg, TN_DIMS, preferred_element_type=F32)
    y = y_s[...] + dsk_ref[...] * xs
    z = p_ref[:, 0:d_ssd]
    y = y * (z * _sigmoid(z))
    for g in range(SSD_GROUPS):
        gs = slice(g * gw, (g + 1) * gw)
        yg = y[:, gs]
        y_s[:, gs] = yg * lax.rsqrt(jnp.mean(yg * yg, axis=-1, keepdims=True) + RMS_EPS) * snw_ref[:, gs]

    e4 = e4_ref[...]
    if nb == 1:
        m_rows = jnp.broadcast_to(m_ref[0:1, :], (rows, LANES))
    else:
        m_rows = jnp.concatenate([jnp.broadcast_to(m_ref[b:b + 1, :], (q, LANES)) for b in range(nb)], axis=0)
    g_all = tot - cum + pre2
    inter_all = cum + m_rows
    m_new, dec_rows = [], []
    for b in range(nb):
        rb = seq_rows[b]
        b_end_m = tot[b * q:b * q + 1, :] + m_ref[b:b + 1, :]
        mn = jnp.maximum(b_end_m, jnp.max(g_all[rb], axis=0, keepdims=True))
        w_s[rb, :] = jnp.exp(g_all[rb] - mn)
        dec = jnp.exp(b_end_m - mn)
        dec_rows.append(jnp.dot(jnp.broadcast_to(dec, (8, LANES)), e4[0:LANES, :].astype(F32), precision=HIGHEST,
                                preferred_element_type=F32)[0:1, :])
        m_new.append(mn)
    w_full = _expand(w_s[...], e4)
    k_scale = hd ** -0.5
    kw = p_ref[:, off_k:off_k + d_ml] * k_scale * w_full
    for h in range(MLSTM_HEADS):
        hs = slice(h * hd, (h + 1) * hd)
        col = SMALL_F + h
        qh = p_ref[:, off_q + h * hd:off_q + (h + 1) * hd]
        qb = qh.astype(BF16)
        kb = (p_ref[:, off_k + h * hd:off_k + (h + 1) * hd] * k_scale).astype(BF16)
        vb = p_ref[:, off_v + h * hd:off_v + (h + 1) * hd].astype(BF16)
        d = (jnp.broadcast_to(cum[:, col:col + 1], (rows, rows))
             - jnp.broadcast_to(cum_t[col:col + 1, :], (rows, rows))
             + jnp.broadcast_to(ig_t[col:col + 1, :], (rows, rows)))
        d = jnp.where(mask, d, -jnp.inf)
        inter = inter_all[:, col:col + 1]
        m_t = jnp.maximum(inter, jnp.max(d, axis=-1, keepdims=True))
        w_intra = jnp.exp(d - m_t)
        w_inter = jnp.exp(inter - m_t)
        s = lax.dot_general(qb, kb, NT_DIMS, preferred_element_type=F32) * w_intra
        num = jnp.dot(s.astype(BF16), vb, preferred_element_type=F32)
        qc = [jnp.dot(qb[seq_rows[b]], c_ref[b, hs, :].astype(BF16), preferred_element_type=F32)
              for b in range(nb)]
        nf = [jnp.broadcast_to(n_ref[b:b + 1, hs], (q, hd)) for b in range(nb)]
        qc = qc[0] if nb == 1 else jnp.concatenate(qc, axis=0)
        nf = nf[0] if nb == 1 else jnp.concatenate(nf, axis=0)
        qn = jnp.sum(qh * nf, axis=-1, keepdims=True)
        num = num + qc * w_inter
        den = jnp.sum(s, axis=-1, keepdims=True) + w_inter * qn
        den = jnp.maximum(jnp.abs(den), jnp.exp(-m_t))
        hh = num / den
        hh = hh * lax.rsqrt(jnp.mean(hh * hh, axis=-1, keepdims=True) + RMS_EPS) * mnw_ref[:, hs]
        o = p_ref[:, off_o + h * hd:off_o + (h + 1) * hd]
        ml_s[:, hs] = hh * _sigmoid(o)
        for b in range(nb):
            kwb = seq_mask(b, kw[:, hs]).astype(BF16)
            c_ref[b, hs, :] = (jnp.broadcast_to(dec_rows[b][:, hs], (hd, hd)) * c_ref[b, hs, :]
                               + lax.dot_general(kwb, vb, TN_DIMS, preferred_element_type=F32))
    for b in range(nb):
        n_ref[b:b + 1, :] = dec_rows[b] * n_ref[b:b + 1, :] + jnp.sum(kw[seq_rows[b]], axis=0, keepdims=True)
        m_ref[b:b + 1, :] = m_new[b]

    mix = jnp.concatenate([y_s[...].astype(BF16), ml_s[...].astype(BF16)], axis=1)
    x1_ref[...] = x_ref[...] + jnp.dot(mix, wout_ref[...], preferred_element_type=F32)


def _mixer(x2d, nw, w_perm, bsz, seq, cw, cb, sp, dsk, snw, mnw, e16, e16t, e4, wout, s0, cv0, c0, n0, m0):
    t, d = x2d.shape
    n_cols = w_perm.shape[1]
    d_ssd = dsk.shape[1]
    d_ml = mnw.shape[1]
    n_state = s0.shape[2]
    d_conv = cw.shape[1]
    q = min(seq, MIX_ROWS)
    nb = MIX_ROWS // q
    assert seq % q == 0 and bsz % nb == 0 and (nb == 1 or q == seq)
    n_chunks = seq // q
    const = lambda shape: pl.BlockSpec(shape, lambda bi, ci: (0,) * len(shape), pipeline_mode=pl.Buffered(1))
    tok = lambda width: pl.BlockSpec((MIX_ROWS, width), lambda bi, ci: (bi * n_chunks + ci, 0))
    st3 = lambda a: pl.BlockSpec((nb,) + a.shape[1:], lambda bi, ci: (bi, 0, 0))
    st2 = lambda a: pl.BlockSpec((nb,) + a.shape[1:], lambda bi, ci: (bi, 0))
    hbm = pl.BlockSpec(memory_space=pl.ANY)
    body = functools.partial(_mixer_body, nb, d_ssd, d_ml, n_state)
    return pl.pallas_call(
        body,
        grid=(bsz // nb, n_chunks),
        in_specs=[tok(d), const(nw.shape), const(w_perm.shape), const(cw.shape), const(cb.shape), const(sp.shape),
                  const(dsk.shape),
                  const(snw.shape), const(mnw.shape), const(e16.shape), const(e16t.shape), const(e4.shape),
                  const(wout.shape), hbm, hbm, hbm, hbm, hbm],
        out_specs=[tok(d), st3(s0), st3(cv0), st3(c0), st2(n0), st2(m0)],
        out_shape=[jax.ShapeDtypeStruct((t, d), F32)] + [jax.ShapeDtypeStruct(a.shape, F32) for a in (s0, cv0, c0, n0, m0)],
        scratch_shapes=[
            pltpu.VMEM((MIX_ROWS, n_cols), F32),
            pltpu.VMEM((8 + q, d_conv), F32),
            pltpu.VMEM((MIX_ROWS, d_conv), F32),
            pltpu.VMEM((MIX_ROWS, d_ssd), F32),
            pltpu.VMEM((MIX_ROWS, d_ml), F32),
            pltpu.VMEM((MIX_ROWS, LANES), F32),
        ],
        compiler_params=pltpu.CompilerParams(
            dimension_semantics=("arbitrary", "arbitrary"), vmem_limit_bytes=VMEM_LIMIT_BYTES),
        name="mixer",
    )(x2d, nw, w_perm, cw, cb, sp, dsk, snw, mnw, e16, e16t, e4, wout, s0, cv0, c0, n0, m0)


def _sort_network(n):
    pairs = []
    p = 1
    while p < n:
        k = p
        while k >= 1:
            for j in range(k % p, n - k, 2 * k):
                for i in range(min(k, n - j - k)):
                    if (i + j) // (2 * p) == (i + j + k) // (2 * p):
                        pairs.append((i + j, i + j + k))
            k //= 2
        p *= 2
    return pairs


def _compare_exchange(v, i, j):
    v[i], v[j] = jnp.maximum(v[i], v[j]), jnp.minimum(v[i], v[j])


def _bitonic_merge(v):
    v = list(v)
    d = len(v) // 2
    while d >= 1:
        for i in range(len(v)):
            if (i & d) == 0:
                _compare_exchange(v, i, i + d)
        d //= 2
    return v


def _merge_top(top, other):
    n = len(top)
    c = list(top)
    for k in range(n - len(other), n):
        c[k] = jnp.maximum(top[k], other[n - 1 - k])
    return _bitonic_merge(c)


def _column_top(vregs, k):
    v = list(vregs)
    for i, j in _sort_network(k):
        _compare_exchange(v, i, j)
    for shift in (4, 2, 1):
        v = _merge_top(v, [pltpu.roll(x, shift, 0) for x in v])
    return v


def _peer_body(n_keys, x_ref, nw_ref, wq_ref, keys_ref, u_ref, vt_ref,
               fnw_ref, y_ref, xh_s, qt_s, top_s, res_s, cnt_s, p1_s, rank_s, p2_s, act_s, acc_s):
    j = pl.program_id(1)
    tm = x_ref.shape[0]
    n1_per_blk = u_ref.shape[0] // n_keys
    pack = 16

    @pl.when(j == 0)
    def _():
        xh = _rms(x_ref[...], nw_ref[...]).astype(BF16)
        xh_s[...] = xh
        qt_s[...] = lax.dot_general(wq_ref[...], xh, NT_DIMS, preferred_element_type=F32).astype(BF16)
        half = keys_ref.shape[2]
        topk = PEER_TOPK
        lane_tiles = [slice(lt * LANES, (lt + 1) * LANES) for lt in range(tm // LANES)]
        width = lambda i: topk // (i + 1)

        def scores(h, c):
            r0 = (h * 2 + c) * half
            return jnp.dot(keys_ref[h * 2 + c], qt_s[r0:r0 + half, :], preferred_element_type=F32)

        def tiles(sc, ls):
            return [sc[k * 8:(k + 1) * 8, ls] for k in range(n_keys // 8)]

        for h in range(PEER_HEADS):
            for c in range(2):
                sc = scores(h, c)
                for ls in lane_tiles:
                    top = _column_top(tiles(sc, ls), topk)
                    for k in range(topk):
                        top_s[c, k, h:h + 1, ls] = top[k][0:1, :]

        for ls in lane_tiles:
            a1 = [top_s[0, k, :, ls] for k in range(topk)]
            a2 = [top_s[1, k, :, ls] for k in range(topk)]
            top = [a1[i] + a2[0] for i in range(topk)]
            for jc in range(1, topk // 2):
                top = _merge_top(top, [a1[i] + a2[jc] for i in range(width(jc))])
            top = _merge_top(top, [a1[0] + a2[jc] for jc in range(topk // 2, topk)])
            tau = top[topk - 1]
            s_max = a1[0] + a2[0]
            z = jnp.zeros_like(tau)
            for i in range(topk):
                cnt_i = jnp.zeros_like(tau)
                for jc in range(width(i)):
                    cand = a1[i] + a2[jc]
                    sel = cand >= tau
                    cnt_i = cnt_i + jnp.where(sel, 1.0, 0.0)
                    z = z + jnp.where(sel, jnp.exp(cand - s_max), 0.0)
                res_s[i, :, ls] = cnt_i
            res_s[topk, :, ls] = 1.0 / z

        for h in range(PEER_HEADS):
            sc1, sc2 = scores(h, 0), scores(h, 1)
            for ls in lane_tiles:
                rep = lambda row: jnp.broadcast_to(row, (8, LANES))
                a1 = [rep(top_s[0, k, h:h + 1, ls]) for k in range(topk)]
                cnt_i = [rep(res_s[i, h:h + 1, ls]) for i in range(topk)]
                inv_z = rep(res_s[topk, h:h + 1, ls])
                for r, s1v in enumerate(tiles(sc1, ls)):
                    cnt = jnp.zeros_like(s1v)
                    for i in range(topk):
                        cnt = jnp.where(s1v == a1[i], cnt_i[i], cnt)
                    cnt_s[h, r * 8:(r + 1) * 8, ls] = cnt
                    p1_s[h, r * 8:(r + 1) * 8, ls] = jnp.exp(s1v - a1[0]) * inv_z
                a2 = [rep(top_s[1, k, h:h + 1, ls]) for k in range(topk)]
                s2v = tiles(sc2, ls)
                for r in range(0, len(s2v), 2):
                    rank, p2 = [], []
                    for v in s2v[r:r + 2]:
                        rk = jnp.full_like(v, float(topk))
                        for k in reversed(range(topk)):
                            rk = jnp.where(v >= a2[k], float(k), rk)
                        rank.append(rk)
                        p2.append(jnp.exp(v - a2[0]))
                    rank_s[h, r * 8:(r + 2) * 8, ls] = jnp.concatenate(rank, axis=0).astype(BF16)
                    p2_s[h, r * 8:(r + 2) * 8, ls] = jnp.concatenate(p2, axis=0).astype(BF16)
        acc_s[...] = jnp.zeros_like(acc_s)

    a_t = lax.dot_general(u_ref[...], xh_s[...], NT_DIMS, preferred_element_type=F32).astype(BF16)

    def gated_act(at_ref, act_ref):
        for i in range(n1_per_blk):
            n1 = j * n1_per_blk + i
            cnt_b = [jnp.broadcast_to(cnt_s[h, pl.ds(n1, 1), :], (pack, tm)).astype(BF16)
                     for h in range(PEER_HEADS)]
            p1_b = [jnp.broadcast_to(p1_s[h, pl.ds(n1, 1), :], (pack, tm)).astype(BF16)
                    for h in range(PEER_HEADS)]
            for r in range(n_keys // pack):
                rs = slice(r * pack, (r + 1) * pack)
                gate = None
                for h in range(PEER_HEADS):
                    p2 = p2_s[h, rs, :]
                    term = jnp.where(rank_s[h, rs, :] < cnt_b[h], p2, jnp.zeros_like(p2)) * p1_b[h]
                    gate = term if gate is None else gate + term
                es = slice(i * n_keys + r * pack, i * n_keys + (r + 1) * pack)
                a = at_ref[es, :]
                gelu = a * (lax.erf(a * (2.0 ** -0.5)) * 0.5 + 0.5)
                act_ref[es, :] = gelu * gate

    gated_act(a_t, act_s)
    acc_s[...] += jnp.dot(vt_ref[...], act_s[...], preferred_element_type=F32)

    @pl.when(j == pl.num_programs(1) - 1)
    def _():
        y_ref[...] = _rms(x_ref[...] + acc_s[...].T, fnw_ref[...])


def _peer(x1, nw, wq_t, keys, u_b, vt_b, fnw):
    t, d = x1.shape
    n_exp = u_b.shape[0]
    n_keys = keys.shape[1]
    tm = 512 if t % 512 == 0 else MIX_ROWS
    eb = PEER_EXPERT_BLOCK
    assert t % tm == 0 and n_exp % eb == 0 and eb % n_keys == 0 and n_exp == n_keys * n_keys
    assert n_keys == 8 * PEER_TOPK and PEER_HEADS == 8
    once = pl.Buffered(1)
    const = lambda shape: pl.BlockSpec(shape, lambda i, j: (0,) * len(shape), pipeline_mode=once)
    head_f32 = pltpu.VMEM((PEER_HEADS, n_keys, tm), F32)
    head_bf16 = pltpu.VMEM((PEER_HEADS, n_keys, tm), BF16)
    return pl.pallas_call(
        functools.partial(_peer_body, n_keys),
        grid=(t // tm, n_exp // eb),
        in_specs=[
            pl.BlockSpec((tm, d), lambda i, j: (i, 0)),
            const(nw.shape), const(wq_t.shape), const(keys.shape),
            pl.BlockSpec((eb, d), lambda i, j: (j, 0)),
            pl.BlockSpec((d, eb), lambda i, j: (0, j)),
            const(fnw.shape),
        ],
        out_specs=pl.BlockSpec((tm, d), lambda i, j: (i, 0)),
        out_shape=jax.ShapeDtypeStruct((t, d), F32),
        scratch_shapes=[
            pltpu.VMEM((tm, d), BF16),
            pltpu.VMEM((wq_t.shape[0], tm), BF16),
            pltpu.VMEM((2, PEER_TOPK, PEER_HEADS, tm), F32),
            pltpu.VMEM((PEER_TOPK + 1, PEER_HEADS, tm), F32),
            head_f32, head_f32, head_bf16, head_bf16,
            pltpu.VMEM((eb, tm), BF16),
            pltpu.VMEM((d, tm), F32),
        ],
        compiler_params=pltpu.CompilerParams(
            dimension_semantics=("arbitrary", "arbitrary"), vmem_limit_bytes=VMEM_LIMIT_BYTES),
        name="peer",
    )(x1, nw, wq_t, keys, u_b, vt_b, fnw)


def _expander(first_row, n_heads, width):
    r = lax.broadcasted_iota(jnp.int32, (LANES, n_heads * width), 0)
    c = lax.broadcasted_iota(jnp.int32, (LANES, n_heads * width), 1)
    return (r == first_row + c // width).astype(F32)


def _pad_lanes(v, first):
    return jnp.zeros((v.shape[0], LANES), F32).at[:, first:first + v.shape[1]].set(v)


def kernel(x_prompt, x_sample, state_ssd_h, state_ssd_conv, state_mlstm_c, state_mlstm_n, state_mlstm_m,
           norm_mix_w, w_in, conv_w, conv_b, dt_bias, a_log, d_skip, ssd_norm_w, if_bias, mlstm_norm_w,
           w_out, norm_ffn_w, peer_w_q, peer_sub_keys, peer_u, peer_v, final_norm_w):
    depth = w_in.shape[0]
    assert depth == 1, "the PEER kernel fuses the final norm, so it must be the last layer's"
    d_model = x_prompt.shape[-1]
    n_ssd_heads = dt_bias.shape[1]
    d_ssd = ssd_norm_w.shape[1]
    d_ml = mlstm_norm_w.shape[1]
    d_conv = conv_w.shape[2]
    n_state = state_ssd_h.shape[-1]
    hd = d_ml // MLSTM_HEADS
    off_dt = d_ssd + d_conv
    off_q = off_dt + n_ssd_heads
    off_if = off_q + 4 * d_ml
    assert n_ssd_heads <= SMALL_F and d_ssd == n_ssd_heads * SSD_HEAD_DIM

    e16t = _expander(0, n_ssd_heads, SSD_HEAD_DIM).T
    twice = lambda e: jnp.concatenate([e, e], axis=0).astype(BF16)
    e16 = twice(_expander(0, n_ssd_heads, SSD_HEAD_DIM))
    e4 = twice(_expander(SMALL_F, MLSTM_HEADS, hd))

    groups = []
    for x, states in ((x_prompt, None),
                      (x_sample, (state_ssd_h, state_ssd_conv, state_mlstm_c, state_mlstm_n, state_mlstm_m))):
        bsz, seq, _ = x.shape
        if states is None:
            st = [(jnp.zeros((bsz, d_ssd, n_state), F32), jnp.zeros((bsz, 8, d_conv), F32),
                   jnp.zeros((bsz, d_ml, hd), F32), jnp.zeros((bsz, d_ml), F32),
                   jnp.zeros((bsz, LANES), F32)) for _ in range(depth)]
        else:
            h0, cv0, c0, n0, m0 = states
            st = [(h0[l].reshape(bsz, d_ssd, n_state),
                   jnp.pad(cv0[l], ((0, 0), (8 - (SSD_CONV - 1), 0), (0, 0))),
                   c0[l].reshape(bsz, d_ml, hd), n0[l].reshape(bsz, d_ml),
                   _pad_lanes(m0[l], SMALL_F)) for l in range(depth)]
        groups.append([x.reshape(bsz * seq, d_model), bsz, seq, st, []])

    for l in range(depth):
        w = w_in[l].astype(BF16)
        zeros = lambda n: jnp.zeros((d_model, n), BF16)
        gates1 = jnp.concatenate([w[:, off_dt:off_q], zeros(SMALL_F - n_ssd_heads),
                                  w[:, off_if + MLSTM_HEADS:off_if + 2 * MLSTM_HEADS],
                                  zeros(LANES - SMALL_F - MLSTM_HEADS)], axis=1)
        gates2 = jnp.concatenate([zeros(SMALL_F), w[:, off_if:off_if + MLSTM_HEADS],
                                  zeros(LANES - SMALL_F - MLSTM_HEADS)], axis=1)
        w_perm = jnp.concatenate([w[:, 0:off_dt], w[:, off_q:off_if], gates1, gates2], axis=1)
        sp = jnp.zeros((8, LANES), F32)
        sp = sp.at[0, 0:n_ssd_heads].set(dt_bias[l]).at[0, SMALL_F:SMALL_F + MLSTM_HEADS].set(if_bias[l, MLSTM_HEADS:])
        sp = sp.at[1, SMALL_F:SMALL_F + MLSTM_HEADS].set(if_bias[l, :MLSTM_HEADS])
        sp = sp.at[2, 0:n_ssd_heads].set(a_log[l])
        dsk = jnp.repeat(d_skip[l], SSD_HEAD_DIM)[None, :]
        wout = w_out[l].astype(BF16)
        wq_t = peer_w_q[l].astype(BF16).T
        keys = peer_sub_keys[l].reshape((-1,) + peer_sub_keys.shape[-2:]).astype(BF16)
        u_b = peer_u[l].astype(BF16)
        vt_b = peer_v[l].astype(BF16).T
        fnw = final_norm_w[None, :]
        for grp in groups:
            x2d, bsz, seq, st, outs = grp
            x1, s_n, cv_n, c_n, n_n, m_n = _mixer(
                x2d, norm_mix_w[l][None, :], w_perm, bsz, seq, conv_w[l], conv_b[l][None, :], sp, dsk, ssd_norm_w[l][None, :],
                mlstm_norm_w[l][None, :], e16, e16t, e4, wout, *st[l])
            outs.append((s_n, cv_n, c_n, n_n, m_n))
            grp[0] = _peer(x1, norm_ffn_w[l][None, :], wq_t, keys, u_b, vt_b, fnw)

    results = []
    for x2d, bsz, seq, st, outs in groups:
        results.append(x2d.reshape(bsz, seq, d_model))
    state_out = []
    for x2d, bsz, seq, st, outs in groups:
        hs = jnp.stack([o[0].reshape(bsz, n_ssd_heads, SSD_HEAD_DIM, n_state) for o in outs])
        cvs = jnp.stack([o[1][:, 8 - (SSD_CONV - 1):, :] for o in outs])
        cs = jnp.stack([o[2].reshape(bsz, MLSTM_HEADS, hd, hd) for o in outs])
        ns = jnp.stack([o[3].reshape(bsz, MLSTM_HEADS, hd) for o in outs])
        ms = jnp.stack([o[4][:, SMALL_F:SMALL_F + MLSTM_HEADS] for o in outs])
        state_out.append((hs, cvs, cs, ns, ms))
    return (results[0], results[1]) + state_out[0] + state_out[1]
```
